```python
import math
import jax, jax.numpy as jnp
from jax import lax
import numpy as np

D_MODEL = 1024
BATCH = 4
SEQ = 4096
DEPTH = 2

N_MEM = 256
N_BRANCH = 3
BRANCH_W = D_MODEL // 2
LRU_BLOCKS = 4
LRU_BLOCK = BRANCH_W // LRU_BLOCKS
CONV_W = 4
LRU_C = 8.0
GLA_HEADS = 4
GLA_DV = BRANCH_W // GLA_HEADS
GLA_DK = GLA_DV // 2
GLA_KW = GLA_HEADS * GLA_DK
GLA_RANK = 16
GLA_TAU = 16.0
GLA_CHUNK = 64
S5_GROUP = 16
S5_GROUPS = BRANCH_W // S5_GROUP
S5_STATE = 64
XA_HEADS = 4
XA_HD = D_MODEL // XA_HEADS
D_FF = -(-8 * D_MODEL // (3 * 256)) * 256
DN_ALPHA = (2 * DEPTH) ** 0.25
DN_BETA = (8 * DEPTH) ** -0.25
LN_EPS = 1e-5

IN_SIZES = (BRANCH_W, BRANCH_W,
            GLA_KW, GLA_KW, BRANCH_W,
            BRANCH_W, GLA_RANK,
            BRANCH_W,
            N_BRANCH * D_MODEL)
D_IN = sum(IN_SIZES)
IN_SPLITS = tuple(int(s) for s in np.cumsum(IN_SIZES)[:-1])

kernel_name = 'hybrid_rglru_gla_s5_deepnorm'


def layer_norm(x, g, b):
    xf = x.astype(jnp.float32)
    mu = jnp.mean(xf, axis=-1, keepdims=True)
    var = jnp.mean(jnp.square(xf - mu), axis=-1, keepdims=True)
    return ((xf - mu) * lax.rsqrt(var + LN_EPS)).astype(x.dtype) * g + b


def rms_norm(x, g):
    xf = x.astype(jnp.float32)
    return (xf * lax.rsqrt(jnp.mean(xf * xf, axis=-1, keepdims=True) + LN_EPS)).astype(x.dtype) * g


def causal_dwconv(u, w, b):
    L = u.shape[1]
    up = jnp.pad(u, ((0, 0), (CONV_W - 1, 0), (0, 0)))
    out = b + up[:, 0:L] * w[0]
    for k in range(1, CONV_W):
        out = out + up[:, k:k + L] * w[k]
    return out


def _linear_combine(e1, e2):
    a1, b1 = e1
    a2, b2 = e2
    return a1 * a2, a2 * b1 + b2


def rg_lru(u, w_a, b_a, w_i, b_i, lam):
    B_, L, _ = u.shape
    ub = u.reshape(B_, L, LRU_BLOCKS, LRU_BLOCK)
    r = jax.nn.sigmoid(jnp.einsum('blhi,hij->blhj', ub, w_a).reshape(B_, L, BRANCH_W) + b_a)
    i = jax.nn.sigmoid(jnp.einsum('blhi,hij->blhj', ub, w_i).reshape(B_, L, BRANCH_W) + b_i)
    log_a = -LRU_C * r * jax.nn.softplus(-lam)
    a = jnp.exp(log_a)
    mult = jnp.sqrt(-jnp.expm1(2.0 * log_a))
    _, h = lax.associative_scan(_linear_combine, (a, mult * (i * u)), axis=1)
    return h


def gla_chunked(q, k, v, log_f):
    B_, L = q.shape[0], q.shape[1]
    NC = L // GLA_CHUNK

    def chunks(t):
        return t.reshape(B_, NC, GLA_CHUNK, GLA_HEADS, t.shape[-1]).transpose(0, 3, 1, 2, 4)

    q, k, v, g = chunks(q) * GLA_DK ** -0.5, chunks(k), chunks(v), chunks(log_f)
    b = jnp.cumsum(g, axis=3)
    b_last = b[:, :, :, -1:]
    q_dec = q * jnp.exp(b)
    k_inv = k * jnp.exp(-b)
    k_end = k * jnp.exp(b_last - b)
    causal = jnp.tril(jnp.ones((GLA_CHUNK, GLA_CHUNK), dtype=bool))
    att = jnp.where(causal, jnp.einsum('bhnck,bhnsk->bhncs', q_dec, k_inv), 0.0)
    o_intra = jnp.einsum('bhncs,bhnsv->bhncv', att, v)
    d_state = jnp.einsum('bhnsk,bhnsv->bhnkv', k_end, v)
    decay = jnp.exp(b_last[:, :, :, 0])

    def step(S, inp):
        dec, ds = inp
        return dec[..., None] * S + ds, S

    S0 = jnp.zeros((B_, GLA_HEADS, GLA_DK, GLA_DV), q.dtype)
    _, S_prev = lax.scan(step, S0, (jnp.moveaxis(decay, 2, 0), jnp.moveaxis(d_state, 2, 0)))
    S_prev = jnp.moveaxis(S_prev, 0, 2)
    o = o_intra + jnp.einsum('bhnck,bhnkv->bhncv', q_dec, S_prev)
    return o.transpose(0, 2, 3, 1, 4).reshape(B_, L, GLA_HEADS, GLA_DV)


def s5_ssm(u, lam_re, lam_im, log_dt, b_re, b_im, c_re, c_im, d_skip):
    B_, L, _ = u.shape
    dt = jnp.exp(log_dt)[:, None]
    mag = jnp.exp(lam_re * dt)
    ab_re = mag * jnp.cos(lam_im * dt)
    ab_im = mag * jnp.sin(lam_im * dt)
    den = lam_re * lam_re + lam_im * lam_im
    f_re = ((ab_re - 1.0) * lam_re + ab_im * lam_im) / den
    f_im = (ab_im * lam_re - (ab_re - 1.0) * lam_im) / den
    bb_re = f_re[..., None] * b_re - f_im[..., None] * b_im
    bb_im = f_re[..., None] * b_im + f_im[..., None] * b_re
    ug = u.reshape(B_, L, S5_GROUPS, S5_GROUP)
    bu_re = jnp.einsum('blgi,gpi->blgp', ug, bb_re)
    bu_im = jnp.einsum('blgi,gpi->blgp', ug, bb_im)
    a_re = jnp.broadcast_to(ab_re, (1, L, S5_GROUPS, S5_STATE))
    a_im = jnp.broadcast_to(ab_im, (1, L, S5_GROUPS, S5_STATE))

    def combine(e1, e2):
        a1r, a1i, b1r, b1i = e1
        a2r, a2i, b2r, b2i = e2
        return (a2r * a1r - a2i * a1i, a2r * a1i + a2i * a1r,
                a2r * b1r - a2i * b1i + b2r, a2r * b1i + a2i * b1r + b2i)

    _, _, xr, xi = lax.associative_scan(combine, (a_re, a_im, bu_re, bu_im), axis=1)
    y = jnp.einsum('blgp,gip->blgi', xr, c_re) - jnp.einsum('blgp,gip->blgi', xi, c_im)
    return y.reshape(B_, L, BRANCH_W) + d_skip * u


def hybrid_mixer(h, w_in, b_in, conv_w, conv_b, lru_w_a, lru_b_a, lru_w_i, lru_b_i, lru_lambda,
                 gla_w_lr, gla_b_lr, gla_norm_g,
                 s5_lam_re, s5_lam_im, s5_log_dt, s5_b_re, s5_b_im, s5_c_re, s5_c_im, s5_d,
                 s5_w_glu, s5_b_glu, w_branch, w_mix_out):
    B_, L, _ = h.shape
    proj = h @ w_in + b_in
    u_lru, g_lru, q, k, v, og, lr, u_s5, gate_logits = jnp.split(proj, IN_SPLITS, axis=-1)
    y_a = rg_lru(causal_dwconv(u_lru, conv_w, conv_b), lru_w_a, lru_b_a, lru_w_i, lru_b_i,
                 lru_lambda) * jax.nn.gelu(g_lru)
    log_f = jax.nn.log_sigmoid(lr @ gla_w_lr + gla_b_lr) / GLA_TAU
    o = gla_chunked(q.reshape(B_, L, GLA_HEADS, GLA_DK), k.reshape(B_, L, GLA_HEADS, GLA_DK),
                    v.reshape(B_, L, GLA_HEADS, GLA_DV), log_f.reshape(B_, L, GLA_HEADS, GLA_DK))
    y_b = rms_norm(o, gla_norm_g).reshape(B_, L, BRANCH_W) * jax.nn.silu(og)
    z = jax.nn.gelu(s5_ssm(u_s5, s5_lam_re, s5_lam_im, s5_log_dt, s5_b_re, s5_b_im,
                           s5_c_re, s5_c_im, s5_d))
    y_c = z * jax.nn.sigmoid(z @ s5_w_glu + s5_b_glu)
    ys = jnp.stack([y_a, y_b, y_c], axis=2)
    gates = jax.nn.sigmoid(gate_logits.reshape(B_, L, N_BRANCH, D_MODEL))
    merged = jnp.sum(gates * jnp.einsum('blnw,nwd->blnd', ys, w_branch), axis=2)
    return merged @ w_mix_out


def cross_attention(h, mem, w_q, w_kv, w_o):
    B_, L, _ = h.shape
    q = (h @ w_q).reshape(B_, L, XA_HEADS, XA_HD)
    k, v = jnp.split(mem @ w_kv, 2, axis=-1)
    k = k.reshape(B_, -1, XA_HEADS, XA_HD)
    v = v.reshape(B_, -1, XA_HEADS, XA_HD)
    s = jnp.einsum('blhd,bmhd->bhlm', q, k) * XA_HD ** -0.5
    p = jax.nn.softmax(s.astype(jnp.float32), axis=-1).astype(v.dtype)
    o = jnp.einsum('bhlm,bmhd->blhd', p, v).reshape(B_, L, D_MODEL)
    return o @ w_o


def swiglu(h, w_gu, w_down):
    gate, up = jnp.split(h @ w_gu, 2, axis=-1)
    return (jax.nn.silu(gate) * up) @ w_down


def setup_inputs(seed: int = 0) -> dict:
    key = jax.random.key(seed)
    ks = iter(jax.random.split(key, 64))

    def nrm(shape, scale):
        return jax.random.normal(next(ks), shape, jnp.float32) * scale

    def gain(shape):
        return 1.0 + nrm(shape, 0.01)

    N = DEPTH
    u = jax.random.uniform(next(ks), (N, BRANCH_W), jnp.float32, 0.9, 0.999)
    s = u ** (1.0 / LRU_C)
    lru_lambda = jnp.log(s) - jnp.log1p(-s)
    n_idx = jnp.arange(S5_STATE, dtype=jnp.float32)
    log_dt = jax.random.uniform(next(ks), (N, S5_GROUPS), jnp.float32,
                                math.log(1e-3), math.log(1e-1))
    return {
        'x': nrm((BATCH, SEQ, D_MODEL), 1.0),
        'mem': nrm((BATCH, N_MEM, D_MODEL), 1.0),
        'ln_in_g': gain((D_MODEL,)),
        'ln_in_b': nrm((D_MODEL,), 0.01),
        'w_in': nrm((N, D_MODEL, D_IN), D_MODEL ** -0.5),
        'b_in': nrm((N, D_IN), 0.01),
        'lru_conv_w': nrm((N, CONV_W, BRANCH_W), CONV_W ** -0.5),
        'lru_conv_b': nrm((N, BRANCH_W), 0.01),
        'lru_w_a': nrm((N, LRU_BLOCKS, LRU_BLOCK, LRU_BLOCK), LRU_BLOCK ** -0.5),
        'lru_b_a': nrm((N, BRANCH_W), 0.01),
        'lru_w_i': nrm((N, LRU_BLOCKS, LRU_BLOCK, LRU_BLOCK), LRU_BLOCK ** -0.5),
        'lru_b_i': nrm((N, BRANCH_W), 0.01),
        'lru_lambda': lru_lambda,
        'gla_w_lr': nrm((N, GLA_RANK, GLA_KW), GLA_RANK ** -0.5),
        'gla_b_lr': nrm((N, GLA_KW), 0.01),
        'gla_norm_g': gain((N, GLA_DV)),
        's5_lam_re': -0.5 + nrm((N, S5_GROUPS, S5_STATE), 0.01),
        's5_lam_im': math.pi * n_idx + nrm((N, S5_GROUPS, S5_STATE), 0.01),
        's5_log_dt': log_dt,
        's5_b_re': nrm((N, S5_GROUPS, S5_STATE, S5_GROUP), (2 * S5_GROUP) ** -0.5),
        's5_b_im': nrm((N, S5_GROUPS, S5_STATE, S5_GROUP), (2 * S5_GROUP) ** -0.5),
        's5_c_re': nrm((N, S5_GROUPS, S5_GROUP, S5_STATE), (2 * S5_STATE) ** -0.5),
        's5_c_im': nrm((N, S5_GROUPS, S5_GROUP, S5_STATE), (2 * S5_STATE) ** -0.5),
        's5_d': nrm((N, BRANCH_W), 1.0),
        's5_w_glu': nrm((N, BRANCH_W, BRANCH_W), BRANCH_W ** -0.5),
        's5_b_glu': nrm((N, BRANCH_W), 0.01),
        'w_branch': nrm((N, N_BRANCH, BRANCH_W, D_MODEL), BRANCH_W ** -0.5),
        'w_mix_out': nrm((N, D_MODEL, D_MODEL), D_MODEL ** -0.5 * DN_BETA),
        'ln1_g': gain((N, D_MODEL)),
        'ln1_b': nrm((N, D_MODEL), 0.01),
        'xa_w_q': nrm((N, D_MODEL, D_MODEL), D_MODEL ** -0.5),
        'xa_w_kv': nrm((N, D_MODEL, 2 * D_MODEL), D_MODEL ** -0.5),
        'xa_w_o': nrm((N, D_MODEL, D_MODEL), D_MODEL ** -0.5 * DN_BETA),
        'ln2_g': gain((N, D_MODEL)),
        'ln2_b': nrm((N, D_MODEL), 0.01),
        'ffn_w_gu': nrm((N, D_MODEL, 2 * D_FF), D_MODEL ** -0.5),
        'ffn_w_down': nrm((N, D_FF, D_MODEL), D_FF ** -0.5 * DN_BETA),
        'ln3_g': gain((N, D_MODEL)),
        'ln3_b': nrm((N, D_MODEL), 0.01),
    }


def reference(x, mem, ln_in_g, ln_in_b, w_in, b_in, lru_conv_w, lru_conv_b, lru_w_a, lru_b_a,
              lru_w_i, lru_b_i, lru_lambda, gla_w_lr, gla_b_lr, gla_norm_g,
              s5_lam_re, s5_lam_im, s5_log_dt, s5_b_re, s5_b_im, s5_c_re, s5_c_im, s5_d,
              s5_w_glu, s5_b_glu, w_branch, w_mix_out, ln1_g, ln1_b,
              xa_w_q, xa_w_kv, xa_w_o, ln2_g, ln2_b, ffn_w_gu, ffn_w_down, ln3_g, ln3_b):
    h = layer_norm(x, ln_in_g, ln_in_b)
    for l in range(DEPTH):
        mix = hybrid_mixer(h, w_in[l], b_in[l], lru_conv_w[l], lru_conv_b[l], lru_w_a[l],
                           lru_b_a[l], lru_w_i[l], lru_b_i[l], lru_lambda[l],
                           gla_w_lr[l], gla_b_lr[l], gla_norm_g[l],
                           s5_lam_re[l], s5_lam_im[l], s5_log_dt[l], s5_b_re[l], s5_b_im[l],
                           s5_c_re[l], s5_c_im[l], s5_d[l], s5_w_glu[l], s5_b_glu[l],
                           w_branch[l], w_mix_out[l])
        h = layer_norm(DN_ALPHA * h + mix, ln1_g[l], ln1_b[l])
        h = layer_norm(DN_ALPHA * h + cross_attention(h, mem, xa_w_q[l], xa_w_kv[l], xa_w_o[l]),
                       ln2_g[l], ln2_b[l])
        h = layer_norm(DN_ALPHA * h + swiglu(h, ffn_w_gu[l], ffn_w_down[l]), ln3_g[l], ln3_b[l])
    return h
```

```python
import functools
import math

import jax
import jax.numpy as jnp
from jax import lax
from jax.experimental import pallas as pl
from jax.experimental.pallas import tpu as pltpu

D_MODEL = 1024
DEPTH = 2
N_BRANCH = 3
BRANCH_W = D_MODEL // 2
LRU_BLOCKS = 4
LRU_BLOCK = BRANCH_W // LRU_BLOCKS
CONV_W = 4
LRU_C = 8.0
GLA_HEADS = 4
GLA_DV = BRANCH_W // GLA_HEADS
GLA_DK = GLA_DV // 2
GLA_KW = GLA_HEADS * GLA_DK
GLA_RANK = 16
GLA_TAU = 16.0
GLA_CHUNK = 64
S5_GROUP = 16
S5_GROUPS = BRANCH_W // S5_GROUP
S5_STATE = 64
S5_NSTATE = S5_GROUPS * S5_STATE
XA_HEADS = 4
XA_HD = D_MODEL // XA_HEADS
D_FF = -(-8 * D_MODEL // (3 * 256)) * 256
DN_ALPHA = (2 * DEPTH) ** 0.25
LN_EPS = 1e-5

LANES = 128
SUBLANES = 8
VMEM_LIMIT_BYTES = 56 * 1024 * 1024

MIX_TILE = 256
XF_TILE = 512
FF_CHUNK = 256
S5_TILE_GROUPS = LANES // S5_GROUP
S5_TILE_STATE = S5_TILE_GROUPS * S5_STATE
S5_TILES = BRANCH_W // LANES

OFF_U_LRU = 0
OFF_G_LRU = OFF_U_LRU + BRANCH_W
OFF_Q = OFF_G_LRU + BRANCH_W
OFF_K = OFF_Q + GLA_KW
OFF_V = OFF_K + GLA_KW
OFF_OG = OFF_V + BRANCH_W
OFF_U_S5 = OFF_OG + BRANCH_W
OFF_GATES = OFF_U_S5 + BRANCH_W
OFF_LR = OFF_GATES + N_BRANCH * D_MODEL
LR_PAD = LANES
D_IN_PAD = OFF_LR + LR_PAD

BF16 = jnp.bfloat16
F32 = jnp.float32


def _mm(a, b):
    return jnp.dot(a, b, preferred_element_type=F32)


def _mm_nt(a, b):
    return lax.dot_general(a, b, (((1,), (1,)), ((), ())), preferred_element_type=F32)


def _mm_tn(a, b):
    return lax.dot_general(a, b, (((0,), (0,)), ((), ())), preferred_element_type=F32)


def _sigmoid(x):
    return 1.0 / (1.0 + jnp.exp(-x))


def _gelu_tanh(x):
    return x * (0.5 * (1.0 + jnp.tanh(math.sqrt(2.0 / math.pi) * (x + 0.044715 * (x * x * x)))))


def _softplus(x):
    return jnp.maximum(x, 0.0) + jnp.log1p(jnp.exp(-jnp.abs(x)))


def _layer_norm(x, g, b):
    mu = jnp.mean(x, axis=-1, keepdims=True)
    xc = x - mu
    var = jnp.mean(xc * xc, axis=-1, keepdims=True)
    return xc * lax.rsqrt(var + LN_EPS) * g + b


def _s5_prep_kernel(lam_re_ref, lam_im_ref, log_dt_ref, b_re_ref, b_im_ref,
                    a_re_ref, a_im_ref, bb_re_ref, bb_im_ref):
    lam_re = lam_re_ref[...]
    lam_im = lam_im_ref[...]
    dt = jnp.exp(log_dt_ref[...])
    mag = jnp.exp(lam_re * dt)
    ab_re = mag * jnp.cos(lam_im * dt)
    ab_im = mag * jnp.sin(lam_im * dt)
    den = lam_re * lam_re + lam_im * lam_im
    f_re = ((ab_re - 1.0) * lam_re + ab_im * lam_im) / den
    f_im = (ab_im * lam_re - (ab_re - 1.0) * lam_im) / den
    a_re_ref[...] = ab_re
    a_im_ref[...] = ab_im
    b_re = b_re_ref[...]
    b_im = b_im_ref[...]
    fr = f_re[:, None, :]
    fi = f_im[:, None, :]
    bb_re_ref[...] = fr * b_re - fi * b_im
    bb_im_ref[...] = fr * b_im + fi * b_re


def _s5_prep(lam_re, lam_im, log_dt, b_re, b_im):
    n = lam_re.shape[0]
    ng = n * S5_GROUPS
    out = pl.pallas_call(
        _s5_prep_kernel,
        out_shape=(
            jax.ShapeDtypeStruct((ng, S5_STATE), F32),
            jax.ShapeDtypeStruct((ng, S5_STATE), F32),
            jax.ShapeDtypeStruct((ng, S5_GROUP, S5_STATE), F32),
            jax.ShapeDtypeStruct((ng, S5_GROUP, S5_STATE), F32),
        ),
        name="s5_prep",
    )(
        lam_re.reshape(ng, S5_STATE),
        lam_im.reshape(ng, S5_STATE),
        log_dt.reshape(ng, 1),
        jnp.swapaxes(b_re, 2, 3).reshape(ng, S5_GROUP, S5_STATE),
        jnp.swapaxes(b_im, 2, 3).reshape(ng, S5_GROUP, S5_STATE),
    )
    a_re, a_im, bb_re, bb_im = out
    return (a_re.reshape(n, 1, S5_NSTATE), a_im.reshape(n, 1, S5_NSTATE),
            bb_re.reshape(n, S5_GROUPS, S5_GROUP, S5_STATE),
            bb_im.reshape(n, S5_GROUPS, S5_GROUP, S5_STATE))


def _s5_block_matrices(bb_re, bb_im, c_re, c_im):
    n = bb_re.shape[0]
    eye = jnp.eye(S5_TILE_GROUPS, dtype=F32)

    def in_mat(bb):
        x = bb.reshape(n, S5_TILES, S5_TILE_GROUPS, S5_GROUP, S5_STATE)
        return jnp.einsum('nkjip,jm->nkjimp', x, eye).reshape(n, S5_TILES, LANES, S5_TILE_STATE)

    def out_mat(c):
        x = c.reshape(n, S5_TILES, S5_TILE_GROUPS, S5_GROUP, S5_STATE)
        return jnp.einsum('nkjip,jm->nkjpmi', x, eye).reshape(n, S5_TILES, S5_TILE_STATE, LANES)

    b_mat = jnp.concatenate([in_mat(bb_re), in_mat(bb_im)], axis=-1).astype(BF16)
    c_mat = jnp.concatenate([out_mat(c_re), -out_mat(c_im)], axis=-2).astype(BF16)
    return b_mat, c_mat


def _kv_kernel(mem_ref, w_ref, k_ref, v_ref):
    kv = _mm(mem_ref[0].astype(BF16), w_ref[0])
    k_ref[0, 0] = kv[:, :D_MODEL].astype(BF16)
    v_ref[0, 0] = kv[:, D_MODEL:].astype(BF16)


def _kv_proj(mem, w_kv_bf16):
    n = w_kv_bf16.shape[0]
    b, m, _ = mem.shape
    shp = jax.ShapeDtypeStruct((n, b, m, D_MODEL), BF16)
    return pl.pallas_call(
        _kv_kernel,
        out_shape=(shp, shp),
        grid=(n, b),
        in_specs=[
            pl.BlockSpec((1, m, D_MODEL), lambda l, i: (i, 0, 0)),
            pl.BlockSpec((1, D_MODEL, 2 * D_MODEL), lambda l, i: (l, 0, 0)),
        ],
        out_specs=(
            pl.BlockSpec((1, 1, m, D_MODEL), lambda l, i: (l, i, 0, 0)),
            pl.BlockSpec((1, 1, m, D_MODEL), lambda l, i: (l, i, 0, 0)),
        ),
        compiler_params=pltpu.CompilerParams(
            dimension_semantics=("arbitrary", "arbitrary"),
            vmem_limit_bytes=VMEM_LIMIT_BYTES),
        name="kv_proj",
    )(mem, w_kv_bf16)


def _mixer_kernel(pre_norm, tile,
                  h_ref, lnin_g_ref, lnin_b_ref, w_in_ref, b_in_ref,
                  conv_w_ref, conv_b_ref, w_ai_ref, b_a_ref, b_i_ref, lam_ref,
                  w_lr_ref, b_lr_ref, norm_g_ref,
                  s5_b_ref, s5_are_ref, s5_aim_ref, s5_c_ref, s5_d_ref, w_glu_ref, b_glu_ref,
                  w_branch_ref, w_mix_ref, ln_g_ref, ln_b_ref,
                  out_ref,
                  ubuf, a_buf, b_buf, lru_carry, bre, bim, xcar_re, xcar_im, st_ref, o_buf):
    c = pl.program_id(1)

    @pl.when(c == 0)
    def _():
        ubuf[0:SUBLANES, :] = jnp.zeros((SUBLANES, BRANCH_W), F32)
        lru_carry[...] = jnp.zeros_like(lru_carry)
        xcar_re[...] = jnp.zeros_like(xcar_re)
        xcar_im[...] = jnp.zeros_like(xcar_im)
        st_ref[...] = jnp.zeros_like(st_ref)

    h = h_ref[0]
    if pre_norm:
        h = _layer_norm(h, lnin_g_ref[...], lnin_b_ref[...])
    hb = h.astype(BF16)

    def proj(lo, width):
        return _mm(hb, w_in_ref[:, lo:lo + width]) + b_in_ref[:, lo:lo + width]

    u = proj(OFF_U_LRU, BRANCH_W)
    hist = SUBLANES
    ubuf[hist:hist + tile, :] = u
    cw = conv_w_ref[...]
    uc = conv_b_ref[...] + u * cw[CONV_W - 1:CONV_W]
    for kk in range(CONV_W - 1):
        lo = hist - (CONV_W - 1) + kk
        uc = uc + ubuf[lo:lo + tile, :] * cw[kk:kk + 1]
    ubuf[0:hist, :] = ubuf[tile:tile + hist, :]

    ucb = uc.astype(BF16)
    sp = _softplus(-lam_ref[...])
    for hh in range(LRU_BLOCKS):
        sl = slice(hh * LRU_BLOCK, (hh + 1) * LRU_BLOCK)
        ra = _mm(ucb[:, sl], w_ai_ref[hh])
        r = _sigmoid(ra[:, :LRU_BLOCK] + b_a_ref[:, sl])
        ig = _sigmoid(ra[:, LRU_BLOCK:] + b_i_ref[:, sl])
        log_a = (-LRU_C) * r * sp[:, sl]
        a = jnp.exp(log_a)
        mult = jnp.sqrt(-jnp.tanh(log_a) * (a * a + 1.0))
        a_buf[:, sl] = a
        b_buf[:, sl] = mult * (ig * uc[:, sl])

    def lru_step(t, hc):
        hn = a_buf[pl.ds(t, 1), :] * hc + b_buf[pl.ds(t, 1), :]
        b_buf[pl.ds(t, 1), :] = hn
        return hn

    lru_carry[...] = lax.fori_loop(0, tile, lru_step, lru_carry[...], unroll=8)
    y_a = b_buf[...] * _gelu_tanh(proj(OFF_G_LRU, BRANCH_W))

    us = proj(OFF_U_S5, BRANCH_W)
    usb = us.astype(BF16)
    for kt in range(S5_TILES):
        bu = _mm(usb[:, kt * LANES:(kt + 1) * LANES], s5_b_ref[kt])
        sl = slice(kt * S5_TILE_STATE, (kt + 1) * S5_TILE_STATE)
        bre[:, sl] = bu[:, :S5_TILE_STATE]
        bim[:, sl] = bu[:, S5_TILE_STATE:]

    for kt in range(S5_TILES):
        sl = slice(kt * S5_TILE_STATE, (kt + 1) * S5_TILE_STATE)
        ar = s5_are_ref[:, sl]
        ai = s5_aim_ref[:, sl]

        def s5_step(t, car, sl=sl, ar=ar, ai=ai):
            xr, xi = car
            nxr = ar * xr - ai * xi + bre[pl.ds(t, 1), sl]
            nxi = ar * xi + ai * xr + bim[pl.ds(t, 1), sl]
            bre[pl.ds(t, 1), sl] = nxr
            bim[pl.ds(t, 1), sl] = nxi
            return nxr, nxi

        xr, xi = lax.fori_loop(0, tile, s5_step, (xcar_re[:, sl], xcar_im[:, sl]), unroll=8)
        xcar_re[:, sl] = xr
        xcar_im[:, sl] = xi

    ys = []
    for kt in range(S5_TILES):
        sl = slice(kt * S5_TILE_STATE, (kt + 1) * S5_TILE_STATE)
        ys.append(_mm(bre[:, sl].astype(BF16), s5_c_ref[kt, 0:S5_TILE_STATE, :])
                  + _mm(bim[:, sl].astype(BF16), s5_c_ref[kt, S5_TILE_STATE:2 * S5_TILE_STATE, :]))
    y_s5 = jnp.concatenate(ys, axis=-1) + s5_d_ref[...] * us
    z = _gelu_tanh(y_s5)
    y_c = z * _sigmoid(_mm(z.astype(BF16), w_glu_ref[...]) + b_glu_ref[...])

    q = proj(OFF_Q, GLA_KW) * (GLA_DK ** -0.5)
    k = proj(OFF_K, GLA_KW)
    v = proj(OFF_V, BRANCH_W).astype(BF16)
    lr = proj(OFF_LR, LR_PAD)
    zf = _mm(lr.astype(BF16), w_lr_ref[...]) + b_lr_ref[...]
    g = (jnp.minimum(zf, 0.0) - jnp.log1p(jnp.exp(-jnp.abs(zf)))) * (1.0 / GLA_TAU)
    rr = lax.broadcasted_iota(jnp.int32, (tile, tile), 0)
    cc = lax.broadcasted_iota(jnp.int32, (tile, tile), 1)
    tri = jnp.where((cc <= rr) & ((cc // GLA_CHUNK) == (rr // GLA_CHUNK)), 1.0, 0.0).astype(BF16)
    g_hi = g.astype(BF16)
    g_lo = (g - g_hi.astype(F32)).astype(BF16)
    bc = _mm(tri, g_hi) + _mm(tri, g_lo)

    r64 = lax.broadcasted_iota(jnp.int32, (GLA_CHUNK, GLA_CHUNK), 0)
    c64 = lax.broadcasted_iota(jnp.int32, (GLA_CHUNK, GLA_CHUNK), 1)
    causal = c64 <= r64
    lane = lax.broadcasted_iota(jnp.int32, (1, LANES), 1)
    head_mask = [jnp.where((lane // GLA_DK) == e, 1.0, 0.0) for e in range(LANES // GLA_DK)]

    for n in range(tile // GLA_CHUNK):
        rows = slice(n * GLA_CHUNK, (n + 1) * GLA_CHUNK)
        bcn = bc[rows]
        bl = bcn[GLA_CHUNK - 1:GLA_CHUNK]
        q_dec = q[rows] * jnp.exp(bcn)
        k_inv = k[rows] * jnp.exp(-bcn)
        k_end = k[rows] * jnp.exp(bl - bcn)
        decay = jnp.exp(bl)
        for p in range(GLA_KW // LANES):
            lanes = slice(p * LANES, (p + 1) * LANES)
            kinv_p = k_inv[:, lanes].astype(BF16)
            for e in range(LANES // GLA_DK):
                hd = p * (LANES // GLA_DK) + e
                vh = v[rows, hd * GLA_DV:(hd + 1) * GLA_DV]
                qm = (q_dec[:, lanes] * head_mask[e]).astype(BF16)
                kem = (k_end[:, lanes] * head_mask[e]).astype(BF16)
                att = jnp.where(causal, _mm_nt(qm, kinv_p), 0.0)
                o = _mm(att.astype(BF16), vh) + _mm_nt(qm, st_ref[hd].astype(BF16))
                o_buf[rows, hd * GLA_DV:(hd + 1) * GLA_DV] = o
                st_ref[hd] = st_ref[hd] * decay[:, lanes] + _mm_tn(vh, kem)

    og = proj(OFF_OG, BRANCH_W)
    ng = norm_g_ref[...]
    yb = []
    for hd in range(GLA_HEADS):
        oh = o_buf[:, hd * GLA_DV:(hd + 1) * GLA_DV]
        ms = jnp.mean(oh * oh, axis=-1, keepdims=True)
        yb.append(oh * lax.rsqrt(ms + LN_EPS) * ng)
    y_b = jnp.concatenate(yb, axis=-1) * (og * _sigmoid(og))

    merged = None
    for nb, y in enumerate((y_a, y_b, y_c)):
        gate = _sigmoid(proj(OFF_GATES + nb * D_MODEL, D_MODEL))
        term = gate * _mm(y.astype(BF16), w_branch_ref[nb])
        merged = term if merged is None else merged + term
    mix = _mm(merged.astype(BF16), w_mix_ref[...])
    out_ref[0] = _layer_norm(DN_ALPHA * h + mix, ln_g_ref[...], ln_b_ref[...])


def _full_spec(arr):
    nd = arr.ndim
    return pl.BlockSpec(arr.shape, lambda *_: (0,) * nd)


def _mixer(h, pre_norm, weights):
    b, l, _ = h.shape
    tile = MIX_TILE
    kern = functools.partial(_mixer_kernel, pre_norm, tile)
    return pl.pallas_call(
        kern,
        out_shape=jax.ShapeDtypeStruct((b, l, D_MODEL), F32),
        grid=(b, l // tile),
        in_specs=[pl.BlockSpec((1, tile, D_MODEL), lambda i, j: (i, j, 0))]
        + [_full_spec(w) for w in weights],
        out_specs=pl.BlockSpec((1, tile, D_MODEL), lambda i, j: (i, j, 0)),
        scratch_shapes=[
            pltpu.VMEM((tile + 2 * SUBLANES, BRANCH_W), F32),
            pltpu.VMEM((tile, BRANCH_W), F32),
            pltpu.VMEM((tile, BRANCH_W), F32),
            pltpu.VMEM((1, BRANCH_W), F32),
            pltpu.VMEM((tile, S5_NSTATE), F32),
            pltpu.VMEM((tile, S5_NSTATE), F32),
            pltpu.VMEM((1, S5_NSTATE), F32),
            pltpu.VMEM((1, S5_NSTATE), F32),
            pltpu.VMEM((GLA_HEADS, GLA_DV, LANES), F32),
            pltpu.VMEM((tile, BRANCH_W), F32),
        ],
        compiler_params=pltpu.CompilerParams(
            dimension_semantics=("arbitrary", "arbitrary"),
            vmem_limit_bytes=VMEM_LIMIT_BYTES),
        name="mixer",
    )(h, *weights)


def _xf_kernel(tile, h_ref, k_ref, v_ref, wq_ref, wo_ref, ln2_g_ref, ln2_b_ref,
               w_gu_ref, w_down_ref, ln3_g_ref, ln3_b_ref, out_ref, o_buf, act_buf):
    h = h_ref[0]
    qb = _mm(h.astype(BF16), wq_ref[...]).astype(BF16)
    for hd in range(XA_HEADS):
        sl = slice(hd * XA_HD, (hd + 1) * XA_HD)
        s = _mm_nt(qb[:, sl], k_ref[0, :, sl]) * (XA_HD ** -0.5)
        s = s - jnp.max(s, axis=-1, keepdims=True)
        p = jnp.exp(s)
        p = p / jnp.sum(p, axis=-1, keepdims=True)
        o_buf[:, sl] = _mm(p.astype(BF16), v_ref[0, :, sl]).astype(BF16)
    xa = _mm(o_buf[...], wo_ref[...])
    h2 = _layer_norm(DN_ALPHA * h + xa, ln2_g_ref[...], ln2_b_ref[...])

    h2b = h2.astype(BF16)
    for j in range(D_FF // FF_CHUNK):
        gate = _mm(h2b, w_gu_ref[:, j * FF_CHUNK:(j + 1) * FF_CHUNK])
        up = _mm(h2b, w_gu_ref[:, D_FF + j * FF_CHUNK:D_FF + (j + 1) * FF_CHUNK])
        act_buf[:, j * FF_CHUNK:(j + 1) * FF_CHUNK] = ((gate * _sigmoid(gate)) * up).astype(BF16)
    ffn = _mm(act_buf[...], w_down_ref[...])
    out_ref[0] = _layer_norm(DN_ALPHA * h2 + ffn, ln3_g_ref[...], ln3_b_ref[...])


def _xf(h, k_mem, v_mem, weights):
    b, l, _ = h.shape
    m = k_mem.shape[1]
    tile = XF_TILE
    kern = functools.partial(_xf_kernel, tile)
    return pl.pallas_call(
        kern,
        out_shape=jax.ShapeDtypeStruct((b, l, D_MODEL), F32),
        grid=(b, l // tile),
        in_specs=[
            pl.BlockSpec((1, tile, D_MODEL), lambda i, j: (i, j, 0)),
            pl.BlockSpec((1, m, D_MODEL), lambda i, j: (i, 0, 0)),
            pl.BlockSpec((1, m, D_MODEL), lambda i, j: (i, 0, 0)),
        ] + [_full_spec(w) for w in weights],
        out_specs=pl.BlockSpec((1, tile, D_MODEL), lambda i, j: (i, j, 0)),
        scratch_shapes=[
            pltpu.VMEM((tile, D_MODEL), BF16),
            pltpu.VMEM((tile, D_FF), BF16),
        ],
        compiler_params=pltpu.CompilerParams(
            dimension_semantics=("arbitrary", "arbitrary"),
            vmem_limit_bytes=VMEM_LIMIT_BYTES),
        name="xattn_ffn",
    )(h, k_mem, v_mem, *weights)


def _reorder_in_proj(w_in, b_in):
    sizes = (BRANCH_W, BRANCH_W, GLA_KW, GLA_KW, BRANCH_W, BRANCH_W, GLA_RANK, BRANCH_W,
             N_BRANCH * D_MODEL)
    offs = [0]
    for s in sizes:
        offs.append(offs[-1] + s)
    seg = lambda x, i: x[..., offs[i]:offs[i + 1]]
    order = (0, 1, 2, 3, 4, 5, 7, 8, 6)
    pad = lambda x: jnp.pad(x, [(0, 0)] * (x.ndim - 1) + [(0, LR_PAD - GLA_RANK)])
    w = jnp.concatenate([seg(w_in, i) for i in order[:-1]] + [pad(seg(w_in, 6))], axis=-1)
    b = jnp.concatenate([seg(b_in, i) for i in order[:-1]] + [pad(seg(b_in, 6))], axis=-1)
    return w.astype(BF16), b[:, None, :]


def kernel(x, mem, ln_in_g, ln_in_b, w_in, b_in, lru_conv_w, lru_conv_b, lru_w_a, lru_b_a, lru_w_i, lru_b_i, lru_lambda, gla_w_lr, gla_b_lr, gla_norm_g, s5_lam_re, s5_lam_im, s5_log_dt, s5_b_re, s5_b_im, s5_c_re, s5_c_im, s5_d, s5_w_glu, s5_b_glu, w_branch, w_mix_out, ln1_g, ln1_b, xa_w_q, xa_w_kv, xa_w_o, ln2_g, ln2_b, ffn_w_gu, ffn_w_down, ln3_g, ln3_b):
    n = w_in.shape[0]
    row = lambda a: a[:, None, :]

    w_in_r, b_in_r = _reorder_in_proj(w_in, b_in)
    w_ai = jnp.concatenate([lru_w_a, lru_w_i], axis=-1).astype(BF16)
    w_lr = jnp.pad(gla_w_lr, ((0, 0), (0, LR_PAD - GLA_RANK), (0, 0))).astype(BF16)
    a_re, a_im, bb_re, bb_im = _s5_prep(s5_lam_re, s5_lam_im, s5_log_dt, s5_b_re, s5_b_im)
    s5_b_mat, s5_c_mat = _s5_block_matrices(bb_re, bb_im, s5_c_re, s5_c_im)
    k_mem, v_mem = _kv_proj(mem, xa_w_kv.astype(BF16))

    h = x
    for l in range(n):
        mixer_w = (
            ln_in_g[None, :], ln_in_b[None, :], w_in_r[l], b_in_r[l],
            lru_conv_w[l], row(lru_conv_b)[l], w_ai[l], row(lru_b_a)[l], row(lru_b_i)[l],
            row(lru_lambda)[l], w_lr[l], row(gla_b_lr)[l], row(gla_norm_g)[l],
            s5_b_mat[l], a_re[l], a_im[l], s5_c_mat[l], row(s5_d)[l],
            s5_w_glu[l].astype(BF16), row(s5_b_glu)[l],
            w_branch[l].astype(BF16), w_mix_out[l].astype(BF16), row(ln1_g)[l], row(ln1_b)[l],
        )
        h = _mixer(h, l == 0, mixer_w)
        xf_w = (
            xa_w_q[l].astype(BF16), xa_w_o[l].astype(BF16), row(ln2_g)[l], row(ln2_b)[l],
            ffn_w_gu[l].astype(BF16), ffn_w_down[l].astype(BF16), row(ln3_g)[l], row(ln3_b)[l],
        )
        h = _xf(h, k_mem[l], v_mem[l], xf_w)
    return h
```

```python
import functools
import math

import numpy as np
import jax
import jax.numpy as jnp
from jax import lax
from jax.experimental import pallas as pl
from jax.experimental.pallas import tpu as pltpu

D_MODEL = 1024
BATCH = 4
DEPTH = 2
N_BRANCH = 3
BRANCH_W = D_MODEL // 2
LRU_BLOCKS = 4
LRU_BLOCK = BRANCH_W // LRU_BLOCKS
CONV_W = 4
LRU_C = 8.0
GLA_HEADS = 4
GLA_DV = BRANCH_W // GLA_HEADS
GLA_DK = GLA_DV // 2
GLA_KW = GLA_HEADS * GLA_DK
GLA_RANK = 16
GLA_TAU = 16.0
GLA_CHUNK = 64
S5_GROUP = 16
S5_GROUPS = BRANCH_W // S5_GROUP
S5_STATE = 64
S5_NSTATE = S5_GROUPS * S5_STATE
XA_HEADS = 4
XA_HD = D_MODEL // XA_HEADS
D_FF = -(-8 * D_MODEL // (3 * 256)) * 256
DN_ALPHA = (2 * DEPTH) ** 0.25
LN_EPS = 1e-5

LANES = 128
SUBLANES = 8
VMEM_LIMIT_BYTES = 56 * 1024 * 1024

STEPS_PER_VREG = SUBLANES // BATCH
MIX_STEPS = 128
MIX_ROWS = MIX_STEPS * BATCH
GLA_ROWS = GLA_CHUNK * BATCH
CONV_HIST = 2 * SUBLANES
XF_TILE = 512
LN_TILE = 1024
FF_CHUNK = 256
S5_TILE_GROUPS = LANES // S5_GROUP
S5_TILE_STATE = S5_TILE_GROUPS * S5_STATE
S5_TILES = BRANCH_W // LANES

OFF_U_LRU = 0
OFF_G_LRU = OFF_U_LRU + BRANCH_W
OFF_Q = OFF_G_LRU + BRANCH_W
OFF_K = OFF_Q + GLA_KW
OFF_V = OFF_K + GLA_KW
OFF_OG = OFF_V + BRANCH_W
OFF_U_S5 = OFF_OG + BRANCH_W
OFF_GATES = OFF_U_S5 + BRANCH_W
OFF_LR = OFF_GATES + N_BRANCH * D_MODEL
LR_PAD = LANES
D_IN_PAD = OFF_LR + LR_PAD

BF16 = jnp.bfloat16
F32 = jnp.float32


def _mm(a, b):
    return jnp.dot(a, b, preferred_element_type=F32)


def _mm_nt(a, b):
    return lax.dot_general(a, b, (((1,), (1,)), ((), ())), preferred_element_type=F32)


def _mm_tn(a, b):
    return lax.dot_general(a, b, (((0,), (0,)), ((), ())), preferred_element_type=F32)


def _sigmoid(x):
    return 1.0 / (1.0 + jnp.exp(-x))


def _gelu_tanh(x):
    return x * (0.5 * (1.0 + jnp.tanh(math.sqrt(2.0 / math.pi) * (x + 0.044715 * (x * x * x)))))


def _softplus(x):
    return jnp.maximum(x, 0.0) + jnp.log1p(jnp.exp(-jnp.abs(x)))


def _layer_norm(x, g, b):
    mu = jnp.mean(x, axis=-1, keepdims=True)
    xc = x - mu
    var = jnp.mean(xc * xc, axis=-1, keepdims=True)
    return xc * lax.rsqrt(var + LN_EPS) * g + b


def _full_spec(arr):
    nd = arr.ndim
    return pl.BlockSpec(arr.shape, lambda *_: (0,) * nd)


def _layer_spec(arr, layer):
    nd = arr.ndim
    return pl.BlockSpec((None,) + arr.shape[1:], lambda *_: (layer,) + (0,) * (nd - 1),
                        pipeline_mode=pl.Buffered(1))


def _ln_in_kernel(x_ref, g_ref, b_ref, out_ref):
    out_ref[...] = _layer_norm(x_ref[0], g_ref[...], b_ref[...])


def _ln_in(x, g, b):
    bsz, l, d = x.shape
    return pl.pallas_call(
        _ln_in_kernel,
        out_shape=jax.ShapeDtypeStruct((l, bsz * d), F32),
        grid=(bsz, l // LN_TILE),
        in_specs=[pl.BlockSpec((1, LN_TILE, d), lambda i, j: (i, j, 0)),
                  _full_spec(g), _full_spec(b)],
        out_specs=pl.BlockSpec((LN_TILE, d), lambda i, j: (j, i)),
        compiler_params=pltpu.CompilerParams(
            dimension_semantics=("arbitrary", "arbitrary"),
            vmem_limit_bytes=VMEM_LIMIT_BYTES),
        name="ln_in",
    )(x, g, b)


def _s5_prep_kernel(lam_re_ref, lam_im_ref, log_dt_ref, b_re_ref, b_im_ref,
                    a_re_ref, a_im_ref, bb_re_ref, bb_im_ref):
    lam_re = lam_re_ref[...]
    lam_im = lam_im_ref[...]
    dt = jnp.exp(log_dt_ref[...])
    mag = jnp.exp(lam_re * dt)
    ab_re = mag * jnp.cos(lam_im * dt)
    ab_im = mag * jnp.sin(lam_im * dt)
    den = lam_re * lam_re + lam_im * lam_im
    f_re = ((ab_re - 1.0) * lam_re + ab_im * lam_im) / den
    f_im = (ab_im * lam_re - (ab_re - 1.0) * lam_im) / den
    a_re_ref[...] = ab_re
    a_im_ref[...] = ab_im
    b_re = b_re_ref[...]
    b_im = b_im_ref[...]
    fr = f_re[:, None, :]
    fi = f_im[:, None, :]
    bb_re_ref[...] = fr * b_re - fi * b_im
    bb_im_ref[...] = fr * b_im + fi * b_re


def _s5_prep(lam_re, lam_im, log_dt, b_re, b_im):
    n = lam_re.shape[0]
    ng = n * S5_GROUPS
    out = pl.pallas_call(
        _s5_prep_kernel,
        out_shape=(
            jax.ShapeDtypeStruct((ng, S5_STATE), F32),
            jax.ShapeDtypeStruct((ng, S5_STATE), F32),
            jax.ShapeDtypeStruct((ng, S5_GROUP, S5_STATE), F32),
            jax.ShapeDtypeStruct((ng, S5_GROUP, S5_STATE), F32),
        ),
        name="s5_prep",
    )(
        lam_re.reshape(ng, S5_STATE),
        lam_im.reshape(ng, S5_STATE),
        log_dt.reshape(ng, 1),
        jnp.swapaxes(b_re, 2, 3).reshape(ng, S5_GROUP, S5_STATE),
        jnp.swapaxes(b_im, 2, 3).reshape(ng, S5_GROUP, S5_STATE),
    )
    a_re, a_im, bb_re, bb_im = out
    return (a_re.reshape(n, 1, S5_NSTATE), a_im.reshape(n, 1, S5_NSTATE),
            bb_re.reshape(n, S5_GROUPS, S5_GROUP, S5_STATE),
            bb_im.reshape(n, S5_GROUPS, S5_GROUP, S5_STATE))


def _s5_block_matrices(bb_re, bb_im, c_re, c_im):
    n = bb_re.shape[0]
    eye = jnp.eye(S5_TILE_GROUPS, dtype=F32)

    def in_mat(bb):
        x = bb.reshape(n, S5_TILES, S5_TILE_GROUPS, S5_GROUP, S5_STATE)
        return jnp.einsum('nkjip,jm->nkjimp', x, eye).reshape(n, S5_TILES, LANES, S5_TILE_STATE)

    def out_mat(c):
        x = c.reshape(n, S5_TILES, S5_TILE_GROUPS, S5_GROUP, S5_STATE)
        return jnp.einsum('nkjip,jm->nkjpmi', x, eye).reshape(n, S5_TILES, S5_TILE_STATE, LANES)

    b_mat = jnp.concatenate([in_mat(bb_re), in_mat(bb_im)], axis=-1).astype(BF16)
    c_mat = jnp.concatenate([out_mat(c_re), -out_mat(c_im)], axis=-2).astype(BF16)
    return b_mat, c_mat


def _kv_kernel(mem_ref, w_ref, k_ref, v_ref):
    kv = _mm(mem_ref[0].astype(BF16), w_ref[0])
    k_ref[0, 0] = kv[:, :D_MODEL].astype(BF16)
    v_ref[0, 0] = kv[:, D_MODEL:].astype(BF16)


def _kv_proj(mem, w_kv_bf16):
    n = w_kv_bf16.shape[0]
    b, m, _ = mem.shape
    shp = jax.ShapeDtypeStruct((n, b, m, D_MODEL), BF16)
    return pl.pallas_call(
        _kv_kernel,
        out_shape=(shp, shp),
        grid=(n, b),
        in_specs=[
            pl.BlockSpec((1, m, D_MODEL), lambda l, i: (i, 0, 0)),
            pl.BlockSpec((1, D_MODEL, 2 * D_MODEL), lambda l, i: (l, 0, 0)),
        ],
        out_specs=(
            pl.BlockSpec((1, 1, m, D_MODEL), lambda l, i: (l, i, 0, 0)),
            pl.BlockSpec((1, 1, m, D_MODEL), lambda l, i: (l, i, 0, 0)),
        ),
        compiler_params=pltpu.CompilerParams(
            dimension_semantics=("arbitrary", "arbitrary"),
            vmem_limit_bytes=VMEM_LIMIT_BYTES),
        name="kv_proj",
    )(mem, w_kv_bf16)


def _gla_masks():
    r = np.arange(GLA_ROWS)
    same = (r[:, None] % BATCH) == (r[None, :] % BATCH)
    tri = same & (r[None, :] <= r[:, None])
    return tri.astype(np.float32), same.astype(np.float32)


def _dup_last_step(x, hi_rows):
    return jnp.where(hi_rows, x, pltpu.roll(x, BATCH, 0))


def _mixer_kernel(h_ref, tri_bf_ref, tri_f32_ref, same_bf_ref, w_in_ref, b_in_ref,
                  conv_w_ref, conv_b_ref, w_ai_ref, b_a_ref, b_i_ref, lam_ref,
                  w_lr_ref, b_lr_ref, norm_g_ref,
                  s5_b_ref, s5_are_ref, s5_aim_ref, s5_c_ref, s5_d_ref, w_glu_ref, b_glu_ref,
                  w_branch_ref, w_mix_ref, ln_g_ref, ln_b_ref,
                  out_ref,
                  ubuf, a_buf, b_buf, lru_carry, bre, bim, xcar_re, xcar_im, st_ref, o_buf):
    rows_n = MIX_ROWS
    n_vregs = rows_n // SUBLANES

    @pl.when(pl.program_id(0) == 0)
    def _():
        ubuf[0:CONV_HIST, :] = jnp.zeros((CONV_HIST, BRANCH_W), F32)
        lru_carry[...] = jnp.zeros_like(lru_carry)
        xcar_re[...] = jnp.zeros_like(xcar_re)
        xcar_im[...] = jnp.zeros_like(xcar_im)
        st_ref[...] = jnp.zeros_like(st_ref)

    h = h_ref[...]
    hb = h.astype(BF16)

    def proj(lo, width):
        return _mm(hb, w_in_ref[:, lo:lo + width]) + b_in_ref[:, lo:lo + width]

    u = proj(OFF_U_LRU, BRANCH_W)
    ubuf[CONV_HIST:CONV_HIST + rows_n, :] = u
    cw = conv_w_ref[...]
    uc = conv_b_ref[...] + u * cw[CONV_W - 1:CONV_W]
    for kk in range(CONV_W - 1):
        lo = CONV_HIST - (CONV_W - 1 - kk) * BATCH
        uc = uc + ubuf[lo:lo + rows_n, :] * cw[kk:kk + 1]
    ubuf[0:CONV_HIST, :] = ubuf[rows_n:rows_n + CONV_HIST, :]

    ucb = uc.astype(BF16)
    sp = _softplus(-lam_ref[...])
    for hh in range(LRU_BLOCKS):
        sl = slice(hh * LRU_BLOCK, (hh + 1) * LRU_BLOCK)
        ra = _mm(ucb[:, sl], w_ai_ref[hh])
        r = _sigmoid(ra[:, :LRU_BLOCK] + b_a_ref[:, sl])
        ig = _sigmoid(ra[:, LRU_BLOCK:] + b_i_ref[:, sl])
        log_a = (-LRU_C) * r * sp[:, sl]
        a = jnp.exp(log_a)
        mult = jnp.sqrt(-jnp.tanh(log_a) * (a * a + 1.0))
        a_buf[:, sl] = a
        b_buf[:, sl] = mult * (ig * uc[:, sl])

    hi_w = lax.broadcasted_iota(jnp.int32, (SUBLANES, BRANCH_W), 0) >= BATCH

    def lru_step(vi, carry):
        rows = pl.ds(pl.multiple_of(vi * SUBLANES, SUBLANES), SUBLANES)
        a = a_buf[rows, :]
        b = b_buf[rows, :]
        z = b + jnp.where(hi_w, a * pltpu.roll(b, BATCH, 0), 0.0)
        acc = a * jnp.where(hi_w, pltpu.roll(a, BATCH, 0), 1.0)
        x = z + acc * carry
        b_buf[rows, :] = x
        return _dup_last_step(x, hi_w)

    lru_carry[...] = lax.fori_loop(0, n_vregs, lru_step, lru_carry[...], unroll=4)
    y_a = b_buf[...] * _gelu_tanh(proj(OFF_G_LRU, BRANCH_W))

    us = proj(OFF_U_S5, BRANCH_W)
    usb = us.astype(BF16)
    for kt in range(S5_TILES):
        bu = _mm(usb[:, kt * LANES:(kt + 1) * LANES], s5_b_ref[kt])
        sl = slice(kt * S5_TILE_STATE, (kt + 1) * S5_TILE_STATE)
        bre[:, sl] = bu[:, :S5_TILE_STATE]
        bim[:, sl] = bu[:, S5_TILE_STATE:]

    hi_s = lax.broadcasted_iota(jnp.int32, (SUBLANES, S5_TILE_STATE), 0) >= BATCH
    for kt in range(S5_TILES):
        sl = slice(kt * S5_TILE_STATE, (kt + 1) * S5_TILE_STATE)
        ar = jnp.broadcast_to(s5_are_ref[:, sl], (SUBLANES, S5_TILE_STATE))
        ai = jnp.broadcast_to(s5_aim_ref[:, sl], (SUBLANES, S5_TILE_STATE))
        am_r = jnp.where(hi_s, ar, 0.0)
        am_i = jnp.where(hi_s, ai, 0.0)
        p_r = jnp.where(hi_s, ar * ar - ai * ai, ar)
        p_i = jnp.where(hi_s, 2.0 * (ar * ai), ai)

        def s5_step(vi, car, sl=sl, am_r=am_r, am_i=am_i, p_r=p_r, p_i=p_i):
            cr, ci = car
            rows = pl.ds(pl.multiple_of(vi * SUBLANES, SUBLANES), SUBLANES)
            br = bre[rows, sl]
            bi = bim[rows, sl]
            rr = pltpu.roll(br, BATCH, 0)
            ri = pltpu.roll(bi, BATCH, 0)
            xr = br + (am_r * rr - am_i * ri) + (p_r * cr - p_i * ci)
            xi = bi + (am_r * ri + am_i * rr) + (p_r * ci + p_i * cr)
            bre[rows, sl] = xr
            bim[rows, sl] = xi
            return _dup_last_step(xr, hi_s), _dup_last_step(xi, hi_s)

        cr, ci = lax.fori_loop(0, n_vregs, s5_step, (xcar_re[:, sl], xcar_im[:, sl]), unroll=4)
        xcar_re[:, sl] = cr
        xcar_im[:, sl] = ci

    ys = []
    for kt in range(S5_TILES):
        sl = slice(kt * S5_TILE_STATE, (kt + 1) * S5_TILE_STATE)
        ys.append(_mm(bre[:, sl].astype(BF16), s5_c_ref[kt, 0:S5_TILE_STATE, :])
                  + _mm(bim[:, sl].astype(BF16), s5_c_ref[kt, S5_TILE_STATE:2 * S5_TILE_STATE, :]))
    y_s5 = jnp.concatenate(ys, axis=-1) + s5_d_ref[...] * us
    z = _gelu_tanh(y_s5)
    y_c = z * _sigmoid(_mm(z.astype(BF16), w_glu_ref[...]) + b_glu_ref[...])

    q = proj(OFF_Q, GLA_KW) * (GLA_DK ** -0.5)
    k = proj(OFF_K, GLA_KW)
    v = proj(OFF_V, BRANCH_W).astype(BF16)
    lr = proj(OFF_LR, LR_PAD)
    zf = _mm(lr.astype(BF16), w_lr_ref[...]) + b_lr_ref[...]
    g = (jnp.minimum(zf, 0.0) - jnp.log1p(jnp.exp(-jnp.abs(zf)))) * (1.0 / GLA_TAU)
    g_hi = g.astype(BF16)
    g_lo = (g - g_hi.astype(F32)).astype(BF16)

    tri_bf = tri_bf_ref[...]
    same_bf = same_bf_ref[...]
    lane = lax.broadcasted_iota(jnp.int32, (1, LANES), 1)
    head_mask = [jnp.where((lane // GLA_DK) == e, 1.0, 0.0) for e in range(LANES // GLA_DK)]
    row_b = lax.broadcasted_iota(jnp.int32, (GLA_ROWS, 1), 0) % BATCH
    batch_mask = [jnp.where(row_b == bb, 1.0, 0.0) for bb in range(BATCH)]

    def by_batch(x):
        return jnp.concatenate([(x * batch_mask[bb]).astype(BF16) for bb in range(BATCH)], axis=-1)

    for n in range(rows_n // GLA_ROWS):
        rows = slice(n * GLA_ROWS, (n + 1) * GLA_ROWS)
        bcn = _mm(tri_bf, g_hi[rows]) + _mm(tri_bf, g_lo[rows])
        bln = _mm(same_bf, g_hi[rows]) + _mm(same_bf, g_lo[rows])
        q_dec = q[rows] * jnp.exp(bcn)
        k_inv = k[rows] * jnp.exp(-bcn)
        k_end = k[rows] * jnp.exp(bln - bcn)
        decay = jnp.exp(bln[GLA_ROWS - BATCH:GLA_ROWS])
        for p in range(GLA_KW // LANES):
            lanes = slice(p * LANES, (p + 1) * LANES)
            kinv_p = k_inv[:, lanes].astype(BF16)
            dec_cat = jnp.concatenate([decay[bb:bb + 1, lanes] for bb in range(BATCH)], axis=-1)
            for e in range(LANES // GLA_DK):
                hd = p * (LANES // GLA_DK) + e
                vh = v[rows, hd * GLA_DV:(hd + 1) * GLA_DV]
                qm = q_dec[:, lanes] * head_mask[e]
                kem = k_end[:, lanes] * head_mask[e]
                att = jnp.where(tri_f32_ref[...] > 0.0, _mm_nt(qm.astype(BF16), kinv_p), 0.0)
                o = _mm(att.astype(BF16), vh) + _mm_nt(by_batch(qm), st_ref[hd].astype(BF16))
                o_buf[rows, hd * GLA_DV:(hd + 1) * GLA_DV] = o
                st_ref[hd] = st_ref[hd] * dec_cat + _mm_tn(vh, by_batch(kem))

    og = proj(OFF_OG, BRANCH_W)
    ng = norm_g_ref[...]
    yb = []
    for hd in range(GLA_HEADS):
        oh = o_buf[:, hd * GLA_DV:(hd + 1) * GLA_DV]
        ms = jnp.mean(oh * oh, axis=-1, keepdims=True)
        yb.append(oh * lax.rsqrt(ms + LN_EPS) * ng)
    y_b = jnp.concatenate(yb, axis=-1) * (og * _sigmoid(og))

    merged = None
    for nb, y in enumerate((y_a, y_b, y_c)):
        gate = _sigmoid(proj(OFF_GATES + nb * D_MODEL, D_MODEL))
        term = gate * _mm(y.astype(BF16), w_branch_ref[nb])
        merged = term if merged is None else merged + term
    mix = _mm(merged.astype(BF16), w_mix_ref[...])
    out_ref[...] = _layer_norm(DN_ALPHA * h + mix, ln_g_ref[...], ln_b_ref[...])


def _mixer(h2d, layer, consts, weights):
    rows_total = h2d.shape[0]
    return pl.pallas_call(
        _mixer_kernel,
        out_shape=jax.ShapeDtypeStruct((rows_total, D_MODEL), F32),
        grid=(rows_total // MIX_ROWS,),
        in_specs=[pl.BlockSpec((MIX_ROWS, D_MODEL), lambda j: (j, 0))]
        + [_full_spec(c) for c in consts]
        + [_layer_spec(w, layer) for w in weights],
        out_specs=pl.BlockSpec((MIX_ROWS, D_MODEL), lambda j: (j, 0)),
        scratch_shapes=[
            pltpu.VMEM((MIX_ROWS + CONV_HIST, BRANCH_W), F32),
            pltpu.VMEM((MIX_ROWS, BRANCH_W), F32),
            pltpu.VMEM((MIX_ROWS, BRANCH_W), F32),
            pltpu.VMEM((SUBLANES, BRANCH_W), F32),
            pltpu.VMEM((MIX_ROWS, S5_NSTATE), F32),
            pltpu.VMEM((MIX_ROWS, S5_NSTATE), F32),
            pltpu.VMEM((SUBLANES, S5_NSTATE), F32),
            pltpu.VMEM((SUBLANES, S5_NSTATE), F32),
            pltpu.VMEM((GLA_HEADS, GLA_DV, BATCH * LANES), F32),
            pltpu.VMEM((MIX_ROWS, BRANCH_W), F32),
        ],
        compiler_params=pltpu.CompilerParams(
            dimension_semantics=("arbitrary",),
            vmem_limit_bytes=VMEM_LIMIT_BYTES),
        name="mixer",
    )(h2d, *consts, *weights)


def _xf_kernel(h_ref, k_ref, v_ref, wq_ref, wo_ref, ln2_g_ref, ln2_b_ref,
               w_gu_ref, w_down_ref, ln3_g_ref, ln3_b_ref, out_ref, o_buf, act_buf):
    h = h_ref[...]
    qb = _mm(h.astype(BF16), wq_ref[...]).astype(BF16)
    for hd in range(XA_HEADS):
        sl = slice(hd * XA_HD, (hd + 1) * XA_HD)
        s = _mm_nt(qb[:, sl], k_ref[:, sl]) * (XA_HD ** -0.5)
        s = s - jnp.max(s, axis=-1, keepdims=True)
        p = jnp.exp(s)
        p = p / jnp.sum(p, axis=-1, keepdims=True)
        o_buf[:, sl] = _mm(p.astype(BF16), v_ref[:, sl]).astype(BF16)
    xa = _mm(o_buf[...], wo_ref[...])
    h2 = _layer_norm(DN_ALPHA * h + xa, ln2_g_ref[...], ln2_b_ref[...])

    h2b = h2.astype(BF16)
    for j in range(D_FF // FF_CHUNK):
        gate = _mm(h2b, w_gu_ref[:, j * FF_CHUNK:(j + 1) * FF_CHUNK])
        up = _mm(h2b, w_gu_ref[:, D_FF + j * FF_CHUNK:D_FF + (j + 1) * FF_CHUNK])
        act_buf[:, j * FF_CHUNK:(j + 1) * FF_CHUNK] = ((gate * _sigmoid(gate)) * up).astype(BF16)
    ffn = _mm(act_buf[...], w_down_ref[...])
    res = _layer_norm(DN_ALPHA * h2 + ffn, ln3_g_ref[...], ln3_b_ref[...])
    out_ref[...] = res.reshape(out_ref.shape)


def _xf(h_tm, layer, k_mem, v_mem, weights, batch_major_out):
    l = h_tm.shape[0]
    m = k_mem.shape[2]
    tile = XF_TILE
    if batch_major_out:
        out_shape = jax.ShapeDtypeStruct((BATCH, l, D_MODEL), F32)
        out_spec = pl.BlockSpec((1, tile, D_MODEL), lambda i, j: (i, j, 0))
    else:
        out_shape = jax.ShapeDtypeStruct((l, BATCH * D_MODEL), F32)
        out_spec = pl.BlockSpec((tile, D_MODEL), lambda i, j: (j, i))
    kv_spec = pl.BlockSpec((None, None, m, D_MODEL), lambda i, j: (layer, i, 0, 0))
    return pl.pallas_call(
        _xf_kernel,
        out_shape=out_shape,
        grid=(BATCH, l // tile),
        in_specs=[pl.BlockSpec((tile, D_MODEL), lambda i, j: (j, i)), kv_spec, kv_spec]
        + [_layer_spec(w, layer) for w in weights],
        out_specs=out_spec,
        scratch_shapes=[
            pltpu.VMEM((tile, D_MODEL), BF16),
            pltpu.VMEM((tile, D_FF), BF16),
        ],
        compiler_params=pltpu.CompilerParams(
            dimension_semantics=("arbitrary", "arbitrary"),
            vmem_limit_bytes=VMEM_LIMIT_BYTES),
        name="xattn_ffn",
    )(h_tm, k_mem, v_mem, *weights)


def _reorder_in_proj(w_in, b_in):
    sizes = (BRANCH_W, BRANCH_W, GLA_KW, GLA_KW, BRANCH_W, BRANCH_W, GLA_RANK, BRANCH_W,
             N_BRANCH * D_MODEL)
    offs = [0]
    for s in sizes:
        offs.append(offs[-1] + s)
    seg = lambda x, i: x[..., offs[i]:offs[i + 1]]
    order = (0, 1, 2, 3, 4, 5, 7, 8)
    pad = lambda x: jnp.pad(x, [(0, 0)] * (x.ndim - 1) + [(0, LR_PAD - GLA_RANK)])
    w = jnp.concatenate([seg(w_in, i).astype(BF16) for i in order] + [pad(seg(w_in, 6)).astype(BF16)],
                        axis=-1)
    b = jnp.concatenate([seg(b_in, i) for i in order] + [pad(seg(b_in, 6))], axis=-1)
    return w, b[:, None, :]


def kernel(x, mem, ln_in_g, ln_in_b, w_in, b_in, lru_conv_w, lru_conv_b, lru_w_a, lru_b_a, lru_w_i, lru_b_i, lru_lambda, gla_w_lr, gla_b_lr, gla_norm_g, s5_lam_re, s5_lam_im, s5_log_dt, s5_b_re, s5_b_im, s5_c_re, s5_c_im, s5_d, s5_w_glu, s5_b_glu, w_branch, w_mix_out, ln1_g, ln1_b, xa_w_q, xa_w_kv, xa_w_o, ln2_g, ln2_b, ffn_w_gu, ffn_w_down, ln3_g, ln3_b):
    n = w_in.shape[0]
    bsz, l, d = x.shape
    assert (bsz, d) == (BATCH, D_MODEL) and l % MIX_STEPS == 0 and l % XF_TILE == 0 and l % LN_TILE == 0
    row = lambda a: a[:, None, :]

    w_in_r, b_in_r = _reorder_in_proj(w_in, b_in)
    w_ai = jnp.concatenate([lru_w_a.astype(BF16), lru_w_i.astype(BF16)], axis=-1)
    w_lr = jnp.pad(gla_w_lr, ((0, 0), (0, LR_PAD - GLA_RANK), (0, 0))).astype(BF16)
    a_re, a_im, bb_re, bb_im = _s5_prep(s5_lam_re, s5_lam_im, s5_log_dt, s5_b_re, s5_b_im)
    s5_b_mat, s5_c_mat = _s5_block_matrices(bb_re, bb_im, s5_c_re, s5_c_im)
    k_mem, v_mem = _kv_proj(mem, xa_w_kv.astype(BF16))

    tri, same = _gla_masks()
    consts = (jnp.asarray(tri, BF16), jnp.asarray(tri, F32), jnp.asarray(same, BF16))
    mixer_w = (
        w_in_r, b_in_r, lru_conv_w, row(lru_conv_b), w_ai, row(lru_b_a), row(lru_b_i),
        row(lru_lambda), w_lr, row(gla_b_lr), row(gla_norm_g),
        s5_b_mat, a_re, a_im, s5_c_mat, row(s5_d), s5_w_glu.astype(BF16), row(s5_b_glu),
        w_branch.astype(BF16), w_mix_out.astype(BF16), row(ln1_g), row(ln1_b),
    )
    xf_w = (
        xa_w_q.astype(BF16), xa_w_o.astype(BF16), row(ln2_g), row(ln2_b),
        ffn_w_gu.astype(BF16), ffn_w_down.astype(BF16), row(ln3_g), row(ln3_b),
    )

    h = _ln_in(x, ln_in_g[None, :], ln_in_b[None, :])
    for layer in range(n):
        h = _mixer(h.reshape(l * bsz, d), layer, consts, mixer_w).reshape(l, bsz * d)
        h = _xf(h, layer, k_mem, v_mem, xf_w, batch_major_out=(layer == n - 1))
    return h
```

```python
import functools
import math

import numpy as np
import jax
import jax.numpy as jnp
from jax import lax
from jax.experimental import pallas as pl
from jax.experimental.pallas import tpu as pltpu

D_MODEL = 1024
BATCH = 4
DEPTH = 2
N_BRANCH = 3
BRANCH_W = D_MODEL // 2
LRU_BLOCKS = 4
LRU_BLOCK = BRANCH_W // LRU_BLOCKS
CONV_W = 4
LRU_C = 8.0
GLA_HEADS = 4
GLA_DV = BRANCH_W // GLA_HEADS
GLA_DK = GLA_DV // 2
GLA_KW = GLA_HEADS * GLA_DK
GLA_RANK = 16
GLA_TAU = 16.0
GLA_CHUNK = 64
S5_GROUP = 16
S5_GROUPS = BRANCH_W // S5_GROUP
S5_STATE = 64
S5_NSTATE = S5_GROUPS * S5_STATE
XA_HEADS = 4
XA_HD = D_MODEL // XA_HEADS
D_FF = -(-8 * D_MODEL // (3 * 256)) * 256
DN_ALPHA = (2 * DEPTH) ** 0.25
LN_EPS = 1e-5

LANES = 128
SUBLANES = 8
VMEM_LIMIT_BYTES = 56 * 1024 * 1024

STEPS_PER_VREG = SUBLANES // BATCH
MIX_STEPS = 128
MIX_ROWS = MIX_STEPS * BATCH
GLA_ROWS = GLA_CHUNK * BATCH
CONV_HIST = 2 * SUBLANES
XF_TILE = 512
LN_TILE = 1024
FF_CHUNK = 256
S5_TILE_GROUPS = LANES // S5_GROUP
S5_TILE_STATE = S5_TILE_GROUPS * S5_STATE
S5_TILES = BRANCH_W // LANES

OFF_U_LRU = 0
OFF_G_LRU = OFF_U_LRU + BRANCH_W
OFF_Q = OFF_G_LRU + BRANCH_W
OFF_K = OFF_Q + GLA_KW
OFF_V = OFF_K + GLA_KW
OFF_OG = OFF_V + BRANCH_W
OFF_U_S5 = OFF_OG + BRANCH_W
OFF_GATES = OFF_U_S5 + BRANCH_W
OFF_LR = OFF_GATES + N_BRANCH * D_MODEL
LR_PAD = LANES
D_IN_PAD = OFF_LR + LR_PAD

BF16 = jnp.bfloat16
F32 = jnp.float32


def _mm(a, b):
    return jnp.dot(a, b, preferred_element_type=F32)


def _mm_nt(a, b):
    return lax.dot_general(a, b, (((1,), (1,)), ((), ())), preferred_element_type=F32)


def _mm_tn(a, b):
    return lax.dot_general(a, b, (((0,), (0,)), ((), ())), preferred_element_type=F32)


def _sigmoid(x):
    return 1.0 / (1.0 + jnp.exp(-x))


def _gelu_tanh(x):
    return x * (0.5 * (1.0 + jnp.tanh(math.sqrt(2.0 / math.pi) * (x + 0.044715 * (x * x * x)))))


def _softplus(x):
    return jnp.maximum(x, 0.0) + jnp.log1p(jnp.exp(-jnp.abs(x)))


def _layer_norm(x, g, b):
    mu = jnp.mean(x, axis=-1, keepdims=True)
    xc = x - mu
    var = jnp.mean(xc * xc, axis=-1, keepdims=True)
    return xc * lax.rsqrt(var + LN_EPS) * g + b


def _full_spec(arr):
    nd = arr.ndim
    return pl.BlockSpec(arr.shape, lambda *_: (0,) * nd)


def _layer_spec(arr, layer):
    nd = arr.ndim
    return pl.BlockSpec((None,) + arr.shape[1:], lambda *_: (layer,) + (0,) * (nd - 1),
                        pipeline_mode=pl.Buffered(1))


def _ln_in_kernel(x_ref, g_ref, b_ref, out_ref):
    out_ref[...] = _layer_norm(x_ref[0], g_ref[...], b_ref[...])


def _ln_in(x, g, b):
    bsz, l, d = x.shape
    return pl.pallas_call(
        _ln_in_kernel,
        out_shape=jax.ShapeDtypeStruct((l, bsz * d), F32),
        grid=(bsz, l // LN_TILE),
        in_specs=[pl.BlockSpec((1, LN_TILE, d), lambda i, j: (i, j, 0)),
                  _full_spec(g), _full_spec(b)],
        out_specs=pl.BlockSpec((LN_TILE, d), lambda i, j: (j, i)),
        compiler_params=pltpu.CompilerParams(
            dimension_semantics=("arbitrary", "arbitrary"),
            vmem_limit_bytes=VMEM_LIMIT_BYTES),
        name="ln_in",
    )(x, g, b)


def _s5_prep_kernel(lam_re_ref, lam_im_ref, log_dt_ref, b_re_ref, b_im_ref,
                    a_re_ref, a_im_ref, bb_re_ref, bb_im_ref):
    lam_re = lam_re_ref[...]
    lam_im = lam_im_ref[...]
    dt = jnp.exp(log_dt_ref[...])
    mag = jnp.exp(lam_re * dt)
    ab_re = mag * jnp.cos(lam_im * dt)
    ab_im = mag * jnp.sin(lam_im * dt)
    den = lam_re * lam_re + lam_im * lam_im
    f_re = ((ab_re - 1.0) * lam_re + ab_im * lam_im) / den
    f_im = (ab_im * lam_re - (ab_re - 1.0) * lam_im) / den
    a_re_ref[...] = ab_re
    a_im_ref[...] = ab_im
    b_re = b_re_ref[...]
    b_im = b_im_ref[...]
    fr = f_re[:, None, :]
    fi = f_im[:, None, :]
    bb_re_ref[...] = fr * b_re - fi * b_im
    bb_im_ref[...] = fr * b_im + fi * b_re


def _s5_prep(lam_re, lam_im, log_dt, b_re, b_im):
    n = lam_re.shape[0]
    ng = n * S5_GROUPS
    out = pl.pallas_call(
        _s5_prep_kernel,
        out_shape=(
            jax.ShapeDtypeStruct((ng, S5_STATE), F32),
            jax.ShapeDtypeStruct((ng, S5_STATE), F32),
            jax.ShapeDtypeStruct((ng, S5_GROUP, S5_STATE), F32),
            jax.ShapeDtypeStruct((ng, S5_GROUP, S5_STATE), F32),
        ),
        name="s5_prep",
    )(
        lam_re.reshape(ng, S5_STATE),
        lam_im.reshape(ng, S5_STATE),
        log_dt.reshape(ng, 1),
        jnp.swapaxes(b_re, 2, 3).reshape(ng, S5_GROUP, S5_STATE),
        jnp.swapaxes(b_im, 2, 3).reshape(ng, S5_GROUP, S5_STATE),
    )
    a_re, a_im, bb_re, bb_im = out
    return (a_re.reshape(n, 1, S5_NSTATE), a_im.reshape(n, 1, S5_NSTATE),
            bb_re.reshape(n, S5_GROUPS, S5_GROUP, S5_STATE),
            bb_im.reshape(n, S5_GROUPS, S5_GROUP, S5_STATE))


def _s5_block_matrices(bb_re, bb_im, c_re, c_im):
    n = bb_re.shape[0]
    eye = jnp.eye(S5_TILE_GROUPS, dtype=F32)

    def in_mat(bb):
        x = bb.reshape(n, S5_TILES, S5_TILE_GROUPS, S5_GROUP, S5_STATE)
        return jnp.einsum('nkjip,jm->nkjimp', x, eye).reshape(n, S5_TILES, LANES, S5_TILE_STATE)

    def out_mat(c):
        x = c.reshape(n, S5_TILES, S5_TILE_GROUPS, S5_GROUP, S5_STATE)
        return jnp.einsum('nkjip,jm->nkjpmi', x, eye).reshape(n, S5_TILES, S5_TILE_STATE, LANES)

    b_mat = jnp.concatenate([in_mat(bb_re), in_mat(bb_im)], axis=-1).astype(BF16)
    c_mat = jnp.concatenate([out_mat(c_re), -out_mat(c_im)], axis=-2).astype(BF16)
    return b_mat, c_mat


def _kv_kernel(mem_ref, w_ref, k_ref, v_ref):
    kv = _mm(mem_ref[0].astype(BF16), w_ref[0])
    k_ref[0, 0] = kv[:, :D_MODEL].astype(BF16)
    v_ref[0, 0] = kv[:, D_MODEL:].astype(BF16)


def _kv_proj(mem, w_kv_bf16):
    n = w_kv_bf16.shape[0]
    b, m, _ = mem.shape
    shp = jax.ShapeDtypeStruct((n, b, m, D_MODEL), BF16)
    return pl.pallas_call(
        _kv_kernel,
        out_shape=(shp, shp),
        grid=(n, b),
        in_specs=[
            pl.BlockSpec((1, m, D_MODEL), lambda l, i: (i, 0, 0)),
            pl.BlockSpec((1, D_MODEL, 2 * D_MODEL), lambda l, i: (l, 0, 0)),
        ],
        out_specs=(
            pl.BlockSpec((1, 1, m, D_MODEL), lambda l, i: (l, i, 0, 0)),
            pl.BlockSpec((1, 1, m, D_MODEL), lambda l, i: (l, i, 0, 0)),
        ),
        compiler_params=pltpu.CompilerParams(
            dimension_semantics=("arbitrary", "arbitrary"),
            vmem_limit_bytes=VMEM_LIMIT_BYTES),
        name="kv_proj",
    )(mem, w_kv_bf16)


def _gla_masks():
    r = np.arange(GLA_ROWS)
    same = (r[:, None] % BATCH) == (r[None, :] % BATCH)
    tri = same & (r[None, :] <= r[:, None])
    return tri.astype(np.float32), same.astype(np.float32)


def _dup_last_step(x, hi_rows):
    return jnp.where(hi_rows, x, pltpu.roll(x, BATCH, 0))


def _mixer_kernel(h_ref, tri_bf_ref, tri_f32_ref, same_bf_ref, w_in_ref, b_in_ref,
                  conv_w_ref, conv_b_ref, w_ai_ref, b_a_ref, b_i_ref, lam_ref,
                  w_lr_ref, b_lr_ref, norm_g_ref,
                  s5_b_ref, s5_are_ref, s5_aim_ref, s5_c_ref, s5_d_ref, w_glu_ref, b_glu_ref,
                  w_branch_ref, w_mix_ref, ln_g_ref, ln_b_ref,
                  out_ref,
                  slab, ubuf, a_buf, b_buf, lru_carry, bre, bim, xcar_re, xcar_im, st_ref, o_buf):
    rows_n = MIX_ROWS
    n_vregs = rows_n // SUBLANES

    @pl.when(pl.program_id(0) == 0)
    def _():
        ubuf[0:CONV_HIST, :] = jnp.zeros((CONV_HIST, BRANCH_W), F32)
        lru_carry[...] = jnp.zeros_like(lru_carry)
        xcar_re[...] = jnp.zeros_like(xcar_re)
        xcar_im[...] = jnp.zeros_like(xcar_im)
        st_ref[...] = jnp.zeros_like(st_ref)

    n_slabs = D_MODEL // LANES
    for bb in range(BATCH):
        for kt in range(n_slabs):
            col = bb * D_MODEL + kt * LANES
            slab[kt, pl.ds(bb, MIX_STEPS, stride=BATCH), :] = h_ref[:, col:col + LANES]
    h = jnp.concatenate([slab[kt] for kt in range(n_slabs)], axis=-1)
    hb = h.astype(BF16)

    def proj(lo, width):
        return _mm(hb, w_in_ref[:, lo:lo + width]) + b_in_ref[:, lo:lo + width]

    u = proj(OFF_U_LRU, BRANCH_W)
    ubuf[CONV_HIST:CONV_HIST + rows_n, :] = u
    cw = conv_w_ref[...]
    uc = conv_b_ref[...] + u * cw[CONV_W - 1:CONV_W]
    for kk in range(CONV_W - 1):
        lo = CONV_HIST - (CONV_W - 1 - kk) * BATCH
        uc = uc + ubuf[lo:lo + rows_n, :] * cw[kk:kk + 1]
    ubuf[0:CONV_HIST, :] = ubuf[rows_n:rows_n + CONV_HIST, :]

    ucb = uc.astype(BF16)
    sp = _softplus(-lam_ref[...])
    for hh in range(LRU_BLOCKS):
        sl = slice(hh * LRU_BLOCK, (hh + 1) * LRU_BLOCK)
        ra = _mm(ucb[:, sl], w_ai_ref[hh])
        r = _sigmoid(ra[:, :LRU_BLOCK] + b_a_ref[:, sl])
        ig = _sigmoid(ra[:, LRU_BLOCK:] + b_i_ref[:, sl])
        log_a = (-LRU_C) * r * sp[:, sl]
        a = jnp.exp(log_a)
        mult = jnp.sqrt(-jnp.tanh(log_a) * (a * a + 1.0))
        a_buf[:, sl] = a
        b_buf[:, sl] = mult * (ig * uc[:, sl])

    hi_w = lax.broadcasted_iota(jnp.int32, (SUBLANES, BRANCH_W), 0) >= BATCH

    def lru_step(vi, carry):
        rows = slice(vi * SUBLANES, (vi + 1) * SUBLANES)
        a = a_buf[rows, :]
        b = b_buf[rows, :]
        z = b + jnp.where(hi_w, a * pltpu.roll(b, BATCH, 0), 0.0)
        acc = a * jnp.where(hi_w, pltpu.roll(a, BATCH, 0), 1.0)
        x = z + acc * carry
        b_buf[rows, :] = x
        return _dup_last_step(x, hi_w)

    carry = lru_carry[...]
    for vi in range(n_vregs):
        carry = lru_step(vi, carry)
    lru_carry[...] = carry
    y_a = b_buf[...] * _gelu_tanh(proj(OFF_G_LRU, BRANCH_W))

    us = proj(OFF_U_S5, BRANCH_W)
    usb = us.astype(BF16)
    for kt in range(S5_TILES):
        bu = _mm(usb[:, kt * LANES:(kt + 1) * LANES], s5_b_ref[kt])
        sl = slice(kt * S5_TILE_STATE, (kt + 1) * S5_TILE_STATE)
        bre[:, sl] = bu[:, :S5_TILE_STATE]
        bim[:, sl] = bu[:, S5_TILE_STATE:]

    hi_s = lax.broadcasted_iota(jnp.int32, (SUBLANES, S5_TILE_STATE), 0) >= BATCH
    for kt in range(S5_TILES):
        sl = slice(kt * S5_TILE_STATE, (kt + 1) * S5_TILE_STATE)
        ar = jnp.broadcast_to(s5_are_ref[:, sl], (SUBLANES, S5_TILE_STATE))
        ai = jnp.broadcast_to(s5_aim_ref[:, sl], (SUBLANES, S5_TILE_STATE))
        am_r = jnp.where(hi_s, ar, 0.0)
        am_i = jnp.where(hi_s, ai, 0.0)
        p_r = jnp.where(hi_s, ar * ar - ai * ai, ar)
        p_i = jnp.where(hi_s, 2.0 * (ar * ai), ai)
        cr = xcar_re[:, sl]
        ci = xcar_im[:, sl]
        for vi in range(n_vregs):
            rows = slice(vi * SUBLANES, (vi + 1) * SUBLANES)
            br = bre[rows, sl]
            bi = bim[rows, sl]
            rr = pltpu.roll(br, BATCH, 0)
            ri = pltpu.roll(bi, BATCH, 0)
            xr = br + (am_r * rr - am_i * ri) + (p_r * cr - p_i * ci)
            xi = bi + (am_r * ri + am_i * rr) + (p_r * ci + p_i * cr)
            bre[rows, sl] = xr
            bim[rows, sl] = xi
            cr = _dup_last_step(xr, hi_s)
            ci = _dup_last_step(xi, hi_s)
        xcar_re[:, sl] = cr
        xcar_im[:, sl] = ci

    ys = []
    for kt in range(S5_TILES):
        sl = slice(kt * S5_TILE_STATE, (kt + 1) * S5_TILE_STATE)
        ys.append(_mm(bre[:, sl].astype(BF16), s5_c_ref[kt, 0:S5_TILE_STATE, :])
                  + _mm(bim[:, sl].astype(BF16), s5_c_ref[kt, S5_TILE_STATE:2 * S5_TILE_STATE, :]))
    y_s5 = jnp.concatenate(ys, axis=-1) + s5_d_ref[...] * us
    z = _gelu_tanh(y_s5)
    y_c = z * _sigmoid(_mm(z.astype(BF16), w_glu_ref[...]) + b_glu_ref[...])

    q = proj(OFF_Q, GLA_KW) * (GLA_DK ** -0.5)
    k = proj(OFF_K, GLA_KW)
    v = proj(OFF_V, BRANCH_W).astype(BF16)
    lr = proj(OFF_LR, LR_PAD)
    zf = _mm(lr.astype(BF16), w_lr_ref[...]) + b_lr_ref[...]
    g = (jnp.minimum(zf, 0.0) - jnp.log1p(jnp.exp(-jnp.abs(zf)))) * (1.0 / GLA_TAU)
    g_hi = g.astype(BF16)
    g_lo = (g - g_hi.astype(F32)).astype(BF16)

    tri_bf = tri_bf_ref[...]
    same_bf = same_bf_ref[...]
    lane = lax.broadcasted_iota(jnp.int32, (1, LANES), 1)
    head_mask = [jnp.where((lane // GLA_DK) == e, 1.0, 0.0) for e in range(LANES // GLA_DK)]
    row_b = lax.broadcasted_iota(jnp.int32, (GLA_ROWS, 1), 0) % BATCH
    batch_mask = [jnp.where(row_b == bb, 1.0, 0.0) for bb in range(BATCH)]

    def by_batch(x):
        return jnp.concatenate([(x * batch_mask[bb]).astype(BF16) for bb in range(BATCH)], axis=-1)

    for n in range(rows_n // GLA_ROWS):
        rows = slice(n * GLA_ROWS, (n + 1) * GLA_ROWS)
        bcn = _mm(tri_bf, g_hi[rows]) + _mm(tri_bf, g_lo[rows])
        bln = _mm(same_bf, g_hi[rows]) + _mm(same_bf, g_lo[rows])
        q_dec = q[rows] * jnp.exp(bcn)
        k_inv = k[rows] * jnp.exp(-bcn)
        k_end = k[rows] * jnp.exp(bln - bcn)
        decay = jnp.exp(bln[GLA_ROWS - BATCH:GLA_ROWS])
        for p in range(GLA_KW // LANES):
            lanes = slice(p * LANES, (p + 1) * LANES)
            kinv_p = k_inv[:, lanes].astype(BF16)
            dec_cat = jnp.concatenate([decay[bb:bb + 1, lanes] for bb in range(BATCH)], axis=-1)
            for e in range(LANES // GLA_DK):
                hd = p * (LANES // GLA_DK) + e
                vh = v[rows, hd * GLA_DV:(hd + 1) * GLA_DV]
                qm = q_dec[:, lanes] * head_mask[e]
                kem = k_end[:, lanes] * head_mask[e]
                att = jnp.where(tri_f32_ref[...] > 0.0, _mm_nt(qm.astype(BF16), kinv_p), 0.0)
                o = _mm(att.astype(BF16), vh) + _mm_nt(by_batch(qm), st_ref[hd].astype(BF16))
                o_buf[rows, hd * GLA_DV:(hd + 1) * GLA_DV] = o
                st_ref[hd] = st_ref[hd] * dec_cat + _mm_tn(vh, by_batch(kem))

    og = proj(OFF_OG, BRANCH_W)
    ng = norm_g_ref[...]
    yb = []
    for hd in range(GLA_HEADS):
        oh = o_buf[:, hd * GLA_DV:(hd + 1) * GLA_DV]
        ms = jnp.mean(oh * oh, axis=-1, keepdims=True)
        yb.append(oh * lax.rsqrt(ms + LN_EPS) * ng)
    y_b = jnp.concatenate(yb, axis=-1) * (og * _sigmoid(og))

    merged = None
    for nb, y in enumerate((y_a, y_b, y_c)):
        gate = _sigmoid(proj(OFF_GATES + nb * D_MODEL, D_MODEL))
        term = gate * _mm(y.astype(BF16), w_branch_ref[nb])
        merged = term if merged is None else merged + term
    mix = _mm(merged.astype(BF16), w_mix_ref[...])
    res = _layer_norm(DN_ALPHA * h + mix, ln_g_ref[...], ln_b_ref[...])
    for kt in range(n_slabs):
        slab[kt] = res[:, kt * LANES:(kt + 1) * LANES]
    for bb in range(BATCH):
        for kt in range(n_slabs):
            col = bb * D_MODEL + kt * LANES
            out_ref[:, col:col + LANES] = slab[kt, pl.ds(bb, MIX_STEPS, stride=BATCH), :]


def _mixer(h_tm, layer, consts, weights):
    l = h_tm.shape[0]
    return pl.pallas_call(
        _mixer_kernel,
        out_shape=jax.ShapeDtypeStruct((l, BATCH * D_MODEL), F32),
        grid=(l // MIX_STEPS,),
        in_specs=[pl.BlockSpec((MIX_STEPS, BATCH * D_MODEL), lambda j: (j, 0))]
        + [_full_spec(c) for c in consts]
        + [_layer_spec(w, layer) for w in weights],
        out_specs=pl.BlockSpec((MIX_STEPS, BATCH * D_MODEL), lambda j: (j, 0)),
        scratch_shapes=[
            pltpu.VMEM((D_MODEL // LANES, MIX_ROWS, LANES), F32),
            pltpu.VMEM((MIX_ROWS + CONV_HIST, BRANCH_W), F32),
            pltpu.VMEM((MIX_ROWS, BRANCH_W), F32),
            pltpu.VMEM((MIX_ROWS, BRANCH_W), F32),
            pltpu.VMEM((SUBLANES, BRANCH_W), F32),
            pltpu.VMEM((MIX_ROWS, S5_NSTATE), F32),
            pltpu.VMEM((MIX_ROWS, S5_NSTATE), F32),
            pltpu.VMEM((SUBLANES, S5_NSTATE), F32),
            pltpu.VMEM((SUBLANES, S5_NSTATE), F32),
            pltpu.VMEM((GLA_HEADS, GLA_DV, BATCH * LANES), F32),
            pltpu.VMEM((MIX_ROWS, BRANCH_W), F32),
        ],
        compiler_params=pltpu.CompilerParams(
            dimension_semantics=("arbitrary",),
            vmem_limit_bytes=VMEM_LIMIT_BYTES),
        name="mixer",
    )(h_tm, *consts, *weights)


def _xf_kernel(h_ref, k_ref, v_ref, wq_ref, wo_ref, ln2_g_ref, ln2_b_ref,
               w_gu_ref, w_down_ref, ln3_g_ref, ln3_b_ref, out_ref, o_buf, act_buf):
    h = h_ref[...]
    qb = _mm(h.astype(BF16), wq_ref[...]).astype(BF16)
    for hd in range(XA_HEADS):
        sl = slice(hd * XA_HD, (hd + 1) * XA_HD)
        s = _mm_nt(qb[:, sl], k_ref[:, sl]) * (XA_HD ** -0.5)
        s = s - jnp.max(s, axis=-1, keepdims=True)
        p = jnp.exp(s)
        p = p / jnp.sum(p, axis=-1, keepdims=True)
        o_buf[:, sl] = _mm(p.astype(BF16), v_ref[:, sl]).astype(BF16)
    xa = _mm(o_buf[...], wo_ref[...])
    h2 = _layer_norm(DN_ALPHA * h + xa, ln2_g_ref[...], ln2_b_ref[...])

    h2b = h2.astype(BF16)
    for j in range(D_FF // FF_CHUNK):
        gate = _mm(h2b, w_gu_ref[:, j * FF_CHUNK:(j + 1) * FF_CHUNK])
        up = _mm(h2b, w_gu_ref[:, D_FF + j * FF_CHUNK:D_FF + (j + 1) * FF_CHUNK])
        act_buf[:, j * FF_CHUNK:(j + 1) * FF_CHUNK] = ((gate * _sigmoid(gate)) * up).astype(BF16)
    ffn = _mm(act_buf[...], w_down_ref[...])
    res = _layer_norm(DN_ALPHA * h2 + ffn, ln3_g_ref[...], ln3_b_ref[...])
    out_ref[...] = res.reshape(out_ref.shape)


def _xf(h_tm, layer, k_mem, v_mem, weights, batch_major_out):
    l = h_tm.shape[0]
    m = k_mem.shape[2]
    tile = XF_TILE
    if batch_major_out:
        out_shape = jax.ShapeDtypeStruct((BATCH, l, D_MODEL), F32)
        out_spec = pl.BlockSpec((1, tile, D_MODEL), lambda i, j: (i, j, 0))
    else:
        out_shape = jax.ShapeDtypeStruct((l, BATCH * D_MODEL), F32)
        out_spec = pl.BlockSpec((tile, D_MODEL), lambda i, j: (j, i))
    kv_spec = pl.BlockSpec((None, None, m, D_MODEL), lambda i, j: (layer, i, 0, 0))
    return pl.pallas_call(
        _xf_kernel,
        out_shape=out_shape,
        grid=(BATCH, l // tile),
        in_specs=[pl.BlockSpec((tile, D_MODEL), lambda i, j: (j, i)), kv_spec, kv_spec]
        + [_layer_spec(w, layer) for w in weights],
        out_specs=out_spec,
        scratch_shapes=[
            pltpu.VMEM((tile, D_MODEL), BF16),
            pltpu.VMEM((tile, D_FF), BF16),
        ],
        compiler_params=pltpu.CompilerParams(
            dimension_semantics=("arbitrary", "arbitrary"),
            vmem_limit_bytes=VMEM_LIMIT_BYTES),
        name="xattn_ffn",
    )(h_tm, k_mem, v_mem, *weights)


def _reorder_in_proj(w_in, b_in):
    sizes = (BRANCH_W, BRANCH_W, GLA_KW, GLA_KW, BRANCH_W, BRANCH_W, GLA_RANK, BRANCH_W,
             N_BRANCH * D_MODEL)
    offs = [0]
    for s in sizes:
        offs.append(offs[-1] + s)
    seg = lambda x, i: x[..., offs[i]:offs[i + 1]]
    order = (0, 1, 2, 3, 4, 5, 7, 8)
    pad = lambda x: jnp.pad(x, [(0, 0)] * (x.ndim - 1) + [(0, LR_PAD - GLA_RANK)])
    w = jnp.concatenate([seg(w_in, i).astype(BF16) for i in order] + [pad(seg(w_in, 6)).astype(BF16)],
                        axis=-1)
    b = jnp.concatenate([seg(b_in, i) for i in order] + [pad(seg(b_in, 6))], axis=-1)
    return w, b[:, None, :]


def kernel(x, mem, ln_in_g, ln_in_b, w_in, b_in, lru_conv_w, lru_conv_b, lru_w_a, lru_b_a, lru_w_i, lru_b_i, lru_lambda, gla_w_lr, gla_b_lr, gla_norm_g, s5_lam_re, s5_lam_im, s5_log_dt, s5_b_re, s5_b_im, s5_c_re, s5_c_im, s5_d, s5_w_glu, s5_b_glu, w_branch, w_mix_out, ln1_g, ln1_b, xa_w_q, xa_w_kv, xa_w_o, ln2_g, ln2_b, ffn_w_gu, ffn_w_down, ln3_g, ln3_b):
    n = w_in.shape[0]
    bsz, l, d = x.shape
    assert (bsz, d) == (BATCH, D_MODEL) and l % MIX_STEPS == 0 and l % XF_TILE == 0 and l % LN_TILE == 0
    row = lambda a: a[:, None, :]

    w_in_r, b_in_r = _reorder_in_proj(w_in, b_in)
    w_ai = jnp.concatenate([lru_w_a.astype(BF16), lru_w_i.astype(BF16)], axis=-1)
    w_lr = jnp.pad(gla_w_lr, ((0, 0), (0, LR_PAD - GLA_RANK), (0, 0))).astype(BF16)
    a_re, a_im, bb_re, bb_im = _s5_prep(s5_lam_re, s5_lam_im, s5_log_dt, s5_b_re, s5_b_im)
    s5_b_mat, s5_c_mat = _s5_block_matrices(bb_re, bb_im, s5_c_re, s5_c_im)
    k_mem, v_mem = _kv_proj(mem, xa_w_kv.astype(BF16))

    tri, same = _gla_masks()
    consts = (jnp.asarray(tri, BF16), jnp.asarray(tri, F32), jnp.asarray(same, BF16))
    mixer_w = (
        w_in_r, b_in_r, lru_conv_w, row(lru_conv_b), w_ai, row(lru_b_a), row(lru_b_i),
        row(lru_lambda), w_lr, row(gla_b_lr), row(gla_norm_g),
        s5_b_mat, a_re, a_im, s5_c_mat, row(s5_d), s5_w_glu.astype(BF16), row(s5_b_glu),
        w_branch.astype(BF16), w_mix_out.astype(BF16), row(ln1_g), row(ln1_b),
    )
    xf_w = (
        xa_w_q.astype(BF16), xa_w_o.astype(BF16), row(ln2_g), row(ln2_b),
        ffn_w_gu.astype(BF16), ffn_w_down.astype(BF16), row(ln3_g), row(ln3_b),
    )

    h = _ln_in(x, ln_in_g[None, :], ln_in_b[None, :])
    for layer in range(n):
        h = _mixer(h, layer, consts, mixer_w)
        h = _xf(h, layer, k_mem, v_mem, xf_w, batch_major_out=(layer == n - 1))
    return h
```
